```python
import math
import jax, jax.numpy as jnp
from jax import lax
import numpy as np

D_MODEL = 1024
BATCH = 4
SEQ = 4096
DEPTH = 2

N_MIXERS = 4
MIXER_WIDTH = D_MODEL // N_MIXERS
D_MIX = N_MIXERS * MIXER_WIDTH
GROUPS_PER_MIXER = 4
GROUP_DIM = MIXER_WIDTH // GROUPS_PER_MIXER
D_IN_PROJ = 8 * MIXER_WIDTH
SCONV_K = 3
SGU_CHUNK = 128
CCONV_K = 31
POOL_WINDOWS = (2, 4, 8, 16)
N_XATTN_HEADS = 4
XATTN_HEAD_DIM = D_MODEL // N_XATTN_HEADS
N_MEM = 256
D_FF = 2816
EPS = 1e-6

kernel_name = "hybrid_parallel_group_macaron_decoder"


def rmsnorm(x, g):
    xf = x.astype(jnp.float32)
    y = xf * lax.rsqrt(jnp.mean(xf * xf, axis=-1, keepdims=True) + EPS)
    return (y * g.astype(jnp.float32)).astype(x.dtype)


def layernorm(x, g, b=None):
    xf = x.astype(jnp.float32)
    mu = jnp.mean(xf, axis=-1, keepdims=True)
    var = jnp.mean(jnp.square(xf - mu), axis=-1, keepdims=True)
    y = (xf - mu) * lax.rsqrt(var + EPS) * g.astype(jnp.float32)
    if b is not None:
        y = y + b.astype(jnp.float32)
    return y.astype(x.dtype)


def swiglu_ffn(h, w_in, w_out):
    g, u = jnp.split(h @ w_in, 2, axis=-1)
    return (jax.nn.silu(g) * u) @ w_out


def causal_depthwise_conv(x, w):
    k = w.shape[0]
    return lax.conv_general_dilated(
        x, w[:, None, :].astype(x.dtype), window_strides=(1,), padding=((k - 1, 0),),
        dimension_numbers=("NWC", "WIO", "NWC"), feature_group_count=x.shape[-1])


def mixer_short_conv(bg, cg, xt, conv_w):
    return bg * causal_depthwise_conv(cg * xt, conv_w)


def mixer_spatial_gating(u, v, norm_g, w_s, b_s):
    bsz, s, _ = v.shape
    vn = layernorm(v, norm_g)
    vr = vn.reshape(bsz, s // SGU_CHUNK, SGU_CHUNK, GROUPS_PER_MIXER, GROUP_DIM)
    w_causal = jnp.tril(w_s)
    mixed = jnp.einsum("hts,bcshd->bcthd", w_causal.astype(vr.dtype), vr)
    mixed = mixed + b_s.T[None, None, :, :, None].astype(vr.dtype)
    return u * mixed.reshape(bsz, s, MIXER_WIDTH)


def mixer_conformer_conv(a, g, conv_w, ln_g, ln_b):
    y = a * jax.nn.sigmoid(g)
    y = causal_depthwise_conv(y, conv_w)
    y = layernorm(y, ln_g, ln_b)
    return jax.nn.silu(y)


def mixer_multiscale_pool(w, pool_w, pool_scale):
    bsz, s, _ = w.shape
    wf = w.astype(jnp.float32)
    cs = jnp.cumsum(wf, axis=1)
    pos = jnp.arange(s, dtype=jnp.int32)
    outs = []
    for gi, k in enumerate(POOL_WINDOWS):
        sl = slice(gi * GROUP_DIM, (gi + 1) * GROUP_DIM)
        csg = cs[:, :, sl]
        shifted = jnp.pad(csg, ((0, 0), (k, 0), (0, 0)))[:, :s]
        count = jnp.minimum(pos + 1, k).astype(jnp.float32)[None, :, None]
        outs.append((csg - shifted) / count - wf[:, :, sl])
    pooled = jnp.stack(outs, axis=2).astype(w.dtype)
    y = jnp.einsum("bsgc,gcd->bsgd", pooled, pool_w)
    return y.reshape(bsz, s, MIXER_WIDTH) * pool_scale


def cross_attention(h, m, wq, wkv, wo):
    bsz, s, _ = h.shape
    q = (h @ wq).reshape(bsz, s, N_XATTN_HEADS, XATTN_HEAD_DIM)
    k, v = jnp.split(m @ wkv, 2, axis=-1)
    k = k.reshape(bsz, N_MEM, N_XATTN_HEADS, XATTN_HEAD_DIM)
    v = v.reshape(bsz, N_MEM, N_XATTN_HEADS, XATTN_HEAD_DIM)
    scores = jnp.einsum("bshd,bmhd->bhsm", q, k).astype(jnp.float32) / math.sqrt(XATTN_HEAD_DIM)
    p = jax.nn.softmax(scores, axis=-1).astype(v.dtype)
    o = jnp.einsum("bhsm,bmhd->bshd", p, v).reshape(bsz, s, D_MODEL)
    return o @ wo


def setup_inputs(seed: int = 0) -> dict:
    key = jax.random.key(seed)
    ks = iter(jax.random.split(key, 32))
    L, D, W = DEPTH, D_MODEL, MIXER_WIDTH

    def nrm(shape, fan_in):
        return jax.random.normal(next(ks), shape, jnp.float32) * (fan_in ** -0.5)

    def gain(shape):
        return 1.0 + 0.02 * jax.random.normal(next(ks), shape, jnp.float32)

    def small(shape):
        return 0.02 * jax.random.normal(next(ks), shape, jnp.float32)

    return {
        "x": jax.random.normal(next(ks), (BATCH, SEQ, D), jnp.float32),
        "mem": jax.random.normal(next(ks), (BATCH, N_MEM, D), jnp.float32),
        "norm_ffn1": gain((L, D)),
        "ffn1_w_in": nrm((L, D, 2 * D_FF), D),
        "ffn1_w_out": nrm((L, D_FF, D), D_FF),
        "norm_mix": gain((L, D)),
        "mix_w_in": nrm((L, D, D_IN_PROJ), D),
        "sconv_w": nrm((L, SCONV_K, W), SCONV_K),
        "sgu_norm_g": gain((L, W)),
        "sgu_w": nrm((L, GROUPS_PER_MIXER, SGU_CHUNK, SGU_CHUNK), SGU_CHUNK),
        "sgu_b": gain((L, GROUPS_PER_MIXER, SGU_CHUNK)),
        "cconv_w": nrm((L, CCONV_K, W), CCONV_K),
        "cconv_ln_g": gain((L, W)),
        "cconv_ln_b": small((L, W)),
        "pool_w": nrm((L, len(POOL_WINDOWS), GROUP_DIM, GROUP_DIM), GROUP_DIM),
        "pool_scale": gain((L, W)),
        "mix_w_out": nrm((L, D_MIX, D), D_MIX),
        "norm_xattn": gain((L, D)),
        "norm_mem": gain((L, D)),
        "xattn_wq": nrm((L, D, D), D),
        "xattn_wkv": nrm((L, D, 2 * D), D),
        "xattn_wo": nrm((L, D, D), D),
        "norm_ffn2": gain((L, D)),
        "ffn2_w_in": nrm((L, D, 2 * D_FF), D),
        "ffn2_w_out": nrm((L, D_FF, D), D_FF),
        "norm_final": gain((D,)),
    }


def reference(x, mem, norm_ffn1, ffn1_w_in, ffn1_w_out, norm_mix, mix_w_in, sconv_w,
              sgu_norm_g, sgu_w, sgu_b, cconv_w, cconv_ln_g, cconv_ln_b, pool_w, pool_scale,
              mix_w_out, norm_xattn, norm_mem, xattn_wq, xattn_wkv, xattn_wo,
              norm_ffn2, ffn2_w_in, ffn2_w_out, norm_final):
    W = MIXER_WIDTH
    split_points = [W, 2 * W, 3 * W, 4 * W, 5 * W, 6 * W, 7 * W]
    for l in range(DEPTH):
        x = x + 0.5 * swiglu_ffn(rmsnorm(x, norm_ffn1[l]), ffn1_w_in[l], ffn1_w_out[l])

        h = rmsnorm(x, norm_mix[l])
        z = h @ mix_w_in[l]
        a_b, a_c, a_x, b_u, b_v, c_a, c_g, d_w = jnp.split(z, split_points, axis=-1)
        y_a = mixer_short_conv(a_b, a_c, a_x, sconv_w[l])
        y_b = mixer_spatial_gating(b_u, b_v, sgu_norm_g[l], sgu_w[l], sgu_b[l])
        y_c = mixer_conformer_conv(c_a, c_g, cconv_w[l], cconv_ln_g[l], cconv_ln_b[l])
        y_d = mixer_multiscale_pool(d_w, pool_w[l], pool_scale[l])
        y = jnp.concatenate([y_a, y_b, y_c, y_d], axis=-1)
        x = x + y @ mix_w_out[l]

        x = x + cross_attention(rmsnorm(x, norm_xattn[l]), rmsnorm(mem, norm_mem[l]),
                                xattn_wq[l], xattn_wkv[l], xattn_wo[l])

        x = x + 0.5 * swiglu_ffn(rmsnorm(x, norm_ffn2[l]), ffn2_w_in[l], ffn2_w_out[l])
    return rmsnorm(x, norm_final)
```

```python
import functools
import math

import jax
import jax.numpy as jnp
from jax import lax
from jax.experimental import pallas as pl
from jax.experimental.pallas import tpu as pltpu

D_MODEL = 1024
DEPTH = 2
MIXER_WIDTH = 256
N_GROUPS = 4
GROUP_DIM = 64
SCONV_K = 3
SGU_CHUNK = 128
CCONV_K = 31
POOL_WINDOWS = (2, 4, 8, 16)
N_HEADS = 4
HEAD_DIM = 256
D_FF = 2816
EPS = 1e-6

TOKEN_TILE = 512
FF_CHUNK = 256
CONV_ROWS = 64
SCONV_HALO = 8
CCONV_HALO = 32
POOL_HALO = 32
VMEM_LIMIT = 56 * 1024 * 1024

F32 = jnp.float32
BF16 = jnp.bfloat16


def _dot(a, b):
    return jnp.dot(a, b, preferred_element_type=F32)


def _rms(x, g):
    ms = jnp.mean(x * x, axis=-1, keepdims=True)
    return x * lax.rsqrt(ms + EPS) * g


def _layernorm(x, g, b=None):
    mu = jnp.mean(x, axis=-1, keepdims=True)
    xc = x - mu
    var = jnp.mean(xc * xc, axis=-1, keepdims=True)
    y = xc * lax.rsqrt(var + EPS) * g
    if b is not None:
        y = y + b
    return y


def _silu(x):
    return x * jax.nn.sigmoid(x)


def _ffn_kernel(*refs, final):
    if final:
        x_ref, g_ref, win_ref, wout_ref, gf_ref, o_ref, act_ref = refs
    else:
        x_ref, g_ref, win_ref, wout_ref, o_ref, act_ref = refs
    x = x_ref[...]
    h = _rms(x, g_ref[...]).astype(BF16)
    for c in range(D_FF // FF_CHUNK):
        lo = c * FF_CHUNK
        gate = _dot(h, win_ref[:, lo:lo + FF_CHUNK])
        up = _dot(h, win_ref[:, D_FF + lo:D_FF + lo + FF_CHUNK])
        act_ref[:, lo:lo + FF_CHUNK] = (_silu(gate) * up).astype(BF16)
    y = x + 0.5 * _dot(act_ref[...], wout_ref[...])
    if final:
        y = _rms(y, gf_ref[...])
    o_ref[...] = y


def _resident(shape, index_map):
    return pl.BlockSpec(shape, index_map, pipeline_mode=pl.Buffered(1))


def _ffn(x2d, gain, w_in, w_out, layer, final_gain=None):
    n_tok = x2d.shape[0]
    final = final_gain is not None
    in_specs = [
        pl.BlockSpec((TOKEN_TILE, D_MODEL), lambda i: (i, 0)),
        _resident((None, 1, D_MODEL), lambda i: (layer, 0, 0)),
        _resident((None, D_MODEL, 2 * D_FF), lambda i: (layer, 0, 0)),
        _resident((None, D_FF, D_MODEL), lambda i: (layer, 0, 0)),
    ]
    args = [x2d, gain, w_in, w_out]
    if final:
        in_specs.append(_resident((1, D_MODEL), lambda i: (0, 0)))
        args.append(final_gain)
    return pl.pallas_call(
        functools.partial(_ffn_kernel, final=final),
        out_shape=jax.ShapeDtypeStruct(x2d.shape, F32),
        grid=(n_tok // TOKEN_TILE,),
        in_specs=in_specs,
        out_specs=pl.BlockSpec((TOKEN_TILE, D_MODEL), lambda i: (i, 0)),
        scratch_shapes=[pltpu.VMEM((TOKEN_TILE, D_FF), BF16)],
        compiler_params=pltpu.CompilerParams(
            dimension_semantics=("arbitrary",), vmem_limit_bytes=VMEM_LIMIT),
        name="ffn_final" if final else "ffn",
    )(*args)


def _mixer_kernel(x_ref, g_ref, win_ref, sconv_ref, sgug_ref, sguw_ref, sgub_ref,
                  cconv_ref, clng_ref, clnb_ref, poolw_ref, pscale_ref, wout_ref,
                  o_ref, pa_ref, yc_ref, d0_ref, d1_ref, ycat_ref):
    W = MIXER_WIDTH
    T = TOKEN_TILE
    j = pl.program_id(1)

    @pl.when(j == 0)
    def _():
        pa_ref[0:SCONV_HALO, :] = jnp.zeros((SCONV_HALO, W), F32)
        yc_ref[0:CCONV_HALO, :] = jnp.zeros((CCONV_HALO, W), F32)
        d0_ref[0:POOL_HALO, :] = jnp.zeros((POOL_HALO, W), F32)

    x = x_ref[...]
    h = _rms(x, g_ref[...]).astype(BF16)
    lane = lax.broadcasted_iota(jnp.int32, (1, W), 1)
    group = lane // GROUP_DIM

    za = _dot(h, win_ref[:, 0:3 * W])
    a_b, a_c, a_x = za[:, 0:W], za[:, W:2 * W], za[:, 2 * W:3 * W]
    p = a_c * a_x
    pa_ref[SCONV_HALO:SCONV_HALO + T, :] = p
    conv = p * sconv_ref[SCONV_K - 1:SCONV_K, :]
    for k in range(SCONV_K - 1):
        off = SCONV_HALO - (SCONV_K - 1) + k
        conv = conv + pa_ref[off:off + T, :] * sconv_ref[k:k + 1, :]
    ycat_ref[:, 0:W] = (a_b * conv).astype(BF16)
    pa_ref[0:SCONV_HALO, :] = pa_ref[T:T + SCONV_HALO, :]

    zb = _dot(h, win_ref[:, 3 * W:5 * W])
    b_u, b_v = zb[:, 0:W], zb[:, W:2 * W]
    vn = _layernorm(b_v, sgug_ref[...])
    row = lax.broadcasted_iota(jnp.int32, (SGU_CHUNK, N_GROUPS * SGU_CHUNK), 0)
    col = lax.broadcasted_iota(jnp.int32, (SGU_CHUNK, N_GROUPS * SGU_CHUNK), 1)
    w_causal = jnp.where((col % SGU_CHUNK) <= row, sguw_ref[...], 0.0).astype(BF16)
    bias = jnp.zeros((SGU_CHUNK, W), F32)
    for hh in range(N_GROUPS):
        bias = jnp.where(group == hh, sgub_ref[:, hh:hh + 1], bias)
    for c in range(T // SGU_CHUNK):
        r0 = c * SGU_CHUNK
        vc = vn[r0:r0 + SGU_CHUNK, :]
        stacked = jnp.concatenate(
            [jnp.where(group == hh, vc, 0.0).astype(BF16) for hh in range(N_GROUPS)], axis=0)
        mixed = _dot(w_causal, stacked) + bias
        ycat_ref[r0:r0 + SGU_CHUNK, W:2 * W] = (b_u[r0:r0 + SGU_CHUNK, :] * mixed).astype(BF16)

    zc = _dot(h, win_ref[:, 5 * W:7 * W])
    yc_ref[CCONV_HALO:CCONV_HALO + T, :] = zc[:, 0:W] * jax.nn.sigmoid(zc[:, W:2 * W])
    for r0 in range(0, T, CONV_ROWS):
        acc = jnp.zeros((CONV_ROWS, W), F32)
        for k in range(CCONV_K):
            off = CCONV_HALO - (CCONV_K - 1) + k + r0
            acc = acc + yc_ref[off:off + CONV_ROWS, :] * cconv_ref[k:k + 1, :]
        yln = _layernorm(acc, clng_ref[...], clnb_ref[...])
        ycat_ref[r0:r0 + CONV_ROWS, 2 * W:3 * W] = _silu(yln).astype(BF16)
    yc_ref[0:CCONV_HALO, :] = yc_ref[T:T + CCONV_HALO, :]

    d_w = _dot(h, win_ref[:, 7 * W:8 * W])
    d0_ref[POOL_HALO:POOL_HALO + T, :] = d_w
    bufs = (d0_ref, d1_ref)
    for lvl in range(len(POOL_WINDOWS)):
        src, dst = bufs[lvl % 2], bufs[(lvl + 1) % 2]
        start = 8 * (lvl + 1)
        n = POOL_HALO + T - start
        shift = 2 ** lvl
        cur = src[start:start + n, :]
        back = src[start - shift:start - shift + n, :]
        dst[start:start + n, :] = cur + jnp.where(group >= lvl, back, 0.0)
    wsum = bufs[len(POOL_WINDOWS) % 2][POOL_HALO:POOL_HALO + T, :]
    pos = j * T + lax.broadcasted_iota(jnp.int32, (T, W), 0)
    window = jnp.zeros((1, W), jnp.int32)
    for gi, kw in enumerate(POOL_WINDOWS):
        window = jnp.where(group == gi, kw, window)
    count = jnp.minimum(pos + 1, window).astype(F32)
    pooled = (wsum / count - d_w).astype(BF16)
    y_d = _dot(pooled, poolw_ref[...]) * pscale_ref[...]
    ycat_ref[:, 3 * W:4 * W] = y_d.astype(BF16)
    d0_ref[0:POOL_HALO, :] = d_w[T - POOL_HALO:T, :]

    o_ref[...] = x + _dot(ycat_ref[...], wout_ref[...])


def _mixer(x3d, layer, gain, w_in, sconv_w, sgu_g, sgu_w, sgu_b, cconv_w, cln_g, cln_b,
           pool_w, pool_scale, w_out):
    bsz, seq, _ = x3d.shape
    W = MIXER_WIDTH
    T = TOKEN_TILE

    def per_layer(shape):
        return _resident((None,) + shape, lambda b, j: (layer,) + (0,) * len(shape))

    return pl.pallas_call(
        _mixer_kernel,
        out_shape=jax.ShapeDtypeStruct(x3d.shape, F32),
        grid=(bsz, seq // T),
        in_specs=[
            pl.BlockSpec((None, T, D_MODEL), lambda b, j: (b, j, 0)),
            per_layer((1, D_MODEL)),
            per_layer((D_MODEL, 8 * W)),
            per_layer((SCONV_K, W)),
            per_layer((1, W)),
            per_layer((SGU_CHUNK, N_GROUPS * SGU_CHUNK)),
            per_layer((SGU_CHUNK, N_GROUPS)),
            per_layer((CCONV_K, W)),
            per_layer((1, W)),
            per_layer((1, W)),
            per_layer((W, W)),
            per_layer((1, W)),
            per_layer((4 * W, D_MODEL)),
        ],
        out_specs=pl.BlockSpec((None, T, D_MODEL), lambda b, j: (b, j, 0)),
        scratch_shapes=[
            pltpu.VMEM((SCONV_HALO + T, W), F32),
            pltpu.VMEM((CCONV_HALO + T, W), F32),
            pltpu.VMEM((POOL_HALO + T, W), F32),
            pltpu.VMEM((POOL_HALO + T, W), F32),
            pltpu.VMEM((T, 4 * W), BF16),
        ],
        compiler_params=pltpu.CompilerParams(
            dimension_semantics=("arbitrary", "arbitrary"), vmem_limit_bytes=VMEM_LIMIT),
        name="mixer",
    )(x3d, gain, w_in, sconv_w, sgu_g, sgu_w, sgu_b, cconv_w, cln_g, cln_b,
      pool_w, pool_scale, w_out)


def _kv_kernel(mem_ref, g_ref, wkv_ref, k_ref, v_ref):
    hm = _rms(mem_ref[...], g_ref[...]).astype(BF16)
    k_ref[...] = _dot(hm, wkv_ref[:, 0:D_MODEL]).astype(BF16)
    v_ref[...] = _dot(hm, wkv_ref[:, D_MODEL:2 * D_MODEL]).astype(BF16)


def _kv(mem, gain, wkv):
    bsz, n_mem, _ = mem.shape
    out = jax.ShapeDtypeStruct((DEPTH, bsz, n_mem, D_MODEL), BF16)
    kv_spec = pl.BlockSpec((None, None, n_mem, D_MODEL), lambda l, b: (l, b, 0, 0))
    return pl.pallas_call(
        _kv_kernel,
        out_shape=(out, out),
        grid=(DEPTH, bsz),
        in_specs=[
            pl.BlockSpec((None, n_mem, D_MODEL), lambda l, b: (b, 0, 0)),
            pl.BlockSpec((None, 1, D_MODEL), lambda l, b: (l, 0, 0)),
            pl.BlockSpec((None, D_MODEL, 2 * D_MODEL), lambda l, b: (l, 0, 0)),
        ],
        out_specs=(kv_spec, kv_spec),
        compiler_params=pltpu.CompilerParams(
            dimension_semantics=("arbitrary", "arbitrary"), vmem_limit_bytes=VMEM_LIMIT),
        name="xattn_kv",
    )(mem, gain, wkv)


def _attn_kernel(x_ref, g_ref, wq_ref, k_ref, v_ref, wo_ref, o_ref, ocat_ref):
    x = x_ref[...]
    h = _rms(x, g_ref[...]).astype(BF16)
    scale = 1.0 / math.sqrt(HEAD_DIM)
    for hd in range(N_HEADS):
        cols = slice(hd * HEAD_DIM, (hd + 1) * HEAD_DIM)
        q = _dot(h, wq_ref[:, cols]).astype(BF16)
        s = lax.dot_general(q, k_ref[:, cols], (((1,), (1,)), ((), ())),
                            preferred_element_type=F32) * scale
        e = jnp.exp(s - jnp.max(s, axis=-1, keepdims=True))
        p = (e / jnp.sum(e, axis=-1, keepdims=True)).astype(BF16)
        ocat_ref[:, cols] = _dot(p, v_ref[:, cols]).astype(BF16)
    o_ref[...] = x + _dot(ocat_ref[...], wo_ref[...])


def _attn(x3d, layer, gain, wq, k, v, wo):
    bsz, seq, _ = x3d.shape
    n_mem = k.shape[2]
    T = TOKEN_TILE
    return pl.pallas_call(
        _attn_kernel,
        out_shape=jax.ShapeDtypeStruct(x3d.shape, F32),
        grid=(bsz, seq // T),
        in_specs=[
            pl.BlockSpec((None, T, D_MODEL), lambda b, j: (b, j, 0)),
            _resident((None, 1, D_MODEL), lambda b, j: (layer, 0, 0)),
            _resident((None, D_MODEL, D_MODEL), lambda b, j: (layer, 0, 0)),
            pl.BlockSpec((None, None, n_mem, D_MODEL), lambda b, j: (layer, b, 0, 0)),
            pl.BlockSpec((None, None, n_mem, D_MODEL), lambda b, j: (layer, b, 0, 0)),
            _resident((None, D_MODEL, D_MODEL), lambda b, j: (layer, 0, 0)),
        ],
        out_specs=pl.BlockSpec((None, T, D_MODEL), lambda b, j: (b, j, 0)),
        scratch_shapes=[pltpu.VMEM((T, D_MODEL), BF16)],
        compiler_params=pltpu.CompilerParams(
            dimension_semantics=("arbitrary", "arbitrary"), vmem_limit_bytes=VMEM_LIMIT),
        name="xattn",
    )(x3d, gain, wq, k, v, wo)


def _block_diag(pool_w):
    n_layers, n_g, c, _ = pool_w.shape
    out = jnp.zeros((n_layers, n_g * c, n_g * c), pool_w.dtype)
    for gi in range(n_g):
        out = out.at[:, gi * c:(gi + 1) * c, gi * c:(gi + 1) * c].set(pool_w[:, gi])
    return out


def kernel(x, mem, norm_ffn1, ffn1_w_in, ffn1_w_out, norm_mix, mix_w_in, sconv_w, sgu_norm_g, sgu_w, sgu_b, cconv_w, cconv_ln_g, cconv_ln_b, pool_w, pool_scale, mix_w_out, norm_xattn, norm_mem, xattn_wq, xattn_wkv, xattn_wo, norm_ffn2, ffn2_w_in, ffn2_w_out, norm_final):
    bsz, seq, d = x.shape
    n_layers = norm_ffn1.shape[0]

    def row(a):
        return a.reshape(a.shape[0], 1, a.shape[1])

    ffn1_w_in_b, ffn1_w_out_b = ffn1_w_in.astype(BF16), ffn1_w_out.astype(BF16)
    ffn2_w_in_b, ffn2_w_out_b = ffn2_w_in.astype(BF16), ffn2_w_out.astype(BF16)
    mix_w_in_b, mix_w_out_b = mix_w_in.astype(BF16), mix_w_out.astype(BF16)
    wq_b, wkv_b, wo_b = xattn_wq.astype(BF16), xattn_wkv.astype(BF16), xattn_wo.astype(BF16)
    pool_w_b = _block_diag(pool_w).astype(BF16)
    sgu_w_cat = sgu_w.transpose(0, 2, 1, 3).reshape(n_layers, SGU_CHUNK, N_GROUPS * SGU_CHUNK)
    sgu_b_t = sgu_b.transpose(0, 2, 1)

    k_all, v_all = _kv(mem, row(norm_mem), wkv_b)

    for l in range(n_layers):
        x2d = _ffn(x.reshape(bsz * seq, d), row(norm_ffn1), ffn1_w_in_b, ffn1_w_out_b, l)
        x = x2d.reshape(bsz, seq, d)
        x = _mixer(x, l, row(norm_mix), mix_w_in_b, sconv_w, row(sgu_norm_g), sgu_w_cat,
                   sgu_b_t, cconv_w, row(cconv_ln_g), row(cconv_ln_b), pool_w_b,
                   row(pool_scale), mix_w_out_b)
        x = _attn(x, l, row(norm_xattn), wq_b, k_all, v_all, wo_b)
        last = l == n_layers - 1
        x2d = _ffn(x.reshape(bsz * seq, d), row(norm_ffn2), ffn2_w_in_b, ffn2_w_out_b, l,
                   final_gain=norm_final.reshape(1, d) if last else None)
        x = x2d.reshape(bsz, seq, d)
    return x
```

```python
import functools
import math

import jax
import jax.numpy as jnp
from jax import lax
from jax.experimental import pallas as pl
from jax.experimental.pallas import tpu as pltpu

D_MODEL = 1024
DEPTH = 2
MIXER_WIDTH = 256
N_GROUPS = 4
GROUP_DIM = 64
SCONV_K = 3
SGU_CHUNK = 128
CCONV_K = 31
POOL_WINDOWS = (2, 4, 8, 16)
N_HEADS = 4
HEAD_DIM = 256
D_FF = 2816
EPS = 1e-6

LANES = 128
TOKEN_TILE = 1024
SUB_TILE = 512
FF_CHUNK = 256
CONV_ROWS = 64
SCONV_HALO = 8
CCONV_HALO = 32
POOL_HALO = 32
VMEM_LIMIT = 56 * 1024 * 1024

F32 = jnp.float32
BF16 = jnp.bfloat16
N_SUB = TOKEN_TILE // SUB_TILE
N_HALF = MIXER_WIDTH // LANES


def _dot(a, b):
    return jnp.dot(a, b, preferred_element_type=F32)


def _rms(x, g):
    ms = jnp.mean(x * x, axis=-1, keepdims=True)
    return x * lax.rsqrt(ms + EPS) * g


def _layernorm(x, g, b=None):
    mu = jnp.mean(x, axis=-1, keepdims=True)
    xc = x - mu
    var = jnp.mean(xc * xc, axis=-1, keepdims=True)
    y = xc * lax.rsqrt(var + EPS) * g
    if b is not None:
        y = y + b
    return y


def _silu(x):
    return x * jax.nn.sigmoid(x)


def _halves(v):
    return [v[:, i * LANES:(i + 1) * LANES] for i in range(v.shape[1] // LANES)]


def _ffn_kernel(*refs, final):
    if final:
        x_ref, g_ref, win_ref, wout_ref, gf_ref, o_ref, act_ref = refs
    else:
        x_ref, g_ref, win_ref, wout_ref, o_ref, act_ref = refs
    for sb in range(N_SUB):
        rows = slice(sb * SUB_TILE, (sb + 1) * SUB_TILE)
        x = x_ref[rows, :]
        h = _rms(x, g_ref[...]).astype(BF16)
        for c in range(D_FF // FF_CHUNK):
            lo = c * FF_CHUNK
            gate = _dot(h, win_ref[:, lo:lo + FF_CHUNK])
            up = _dot(h, win_ref[:, D_FF + lo:D_FF + lo + FF_CHUNK])
            act_ref[rows, lo:lo + FF_CHUNK] = (_silu(gate) * up).astype(BF16)
        y = x + 0.5 * _dot(act_ref[rows, :], wout_ref[...])
        if final:
            y = _rms(y, gf_ref[...])
        o_ref[rows, :] = y


def _resident(shape, index_map):
    return pl.BlockSpec(shape, index_map, pipeline_mode=pl.Buffered(1))


def _ffn(x2d, gain, w_in, w_out, layer, final_gain=None):
    n_tok = x2d.shape[0]
    final = final_gain is not None
    in_specs = [
        pl.BlockSpec((TOKEN_TILE, D_MODEL), lambda i: (i, 0)),
        _resident((None, 1, D_MODEL), lambda i: (layer, 0, 0)),
        _resident((None, D_MODEL, 2 * D_FF), lambda i: (layer, 0, 0)),
        _resident((None, D_FF, D_MODEL), lambda i: (layer, 0, 0)),
    ]
    args = [x2d, gain, w_in, w_out]
    if final:
        in_specs.append(_resident((1, D_MODEL), lambda i: (0, 0)))
        args.append(final_gain)
    return pl.pallas_call(
        functools.partial(_ffn_kernel, final=final),
        out_shape=jax.ShapeDtypeStruct(x2d.shape, F32),
        grid=(n_tok // TOKEN_TILE,),
        in_specs=in_specs,
        out_specs=pl.BlockSpec((TOKEN_TILE, D_MODEL), lambda i: (i, 0)),
        scratch_shapes=[pltpu.VMEM((TOKEN_TILE, D_FF), BF16)],
        compiler_params=pltpu.CompilerParams(
            dimension_semantics=("arbitrary",), vmem_limit_bytes=VMEM_LIMIT),
        name="ffn_final" if final else "ffn",
    )(*args)


def _mixer_kernel(x_ref, g_ref, win_ref, sconv_ref, sgug_ref, sguw_ref, sgub_ref,
                  cconv_ref, clng_ref, clnb_ref, poolw_ref, pscale_ref, wout_ref,
                  o_ref, pa_ref, yc_ref, dw_ref, lvl_ref, ycat_ref):
    W = MIXER_WIDTH
    T = TOKEN_TILE
    S = SUB_TILE
    j = pl.program_id(1)

    @pl.when(j == 0)
    def _():
        for hf in range(N_HALF):
            pa_ref[hf, 0:SCONV_HALO, :] = jnp.zeros((SCONV_HALO, LANES), F32)
            yc_ref[hf, 0:CCONV_HALO, :] = jnp.zeros((CCONV_HALO, LANES), F32)
            dw_ref[hf, 0:POOL_HALO, :] = jnp.zeros((POOL_HALO, LANES), F32)

    lane = lax.broadcasted_iota(jnp.int32, (1, W), 1)
    group = lane // GROUP_DIM
    row = lax.broadcasted_iota(jnp.int32, (SGU_CHUNK, N_GROUPS * SGU_CHUNK), 0)
    col = lax.broadcasted_iota(jnp.int32, (SGU_CHUNK, N_GROUPS * SGU_CHUNK), 1)
    w_causal = jnp.where((col % SGU_CHUNK) <= row, sguw_ref[...], 0.0).astype(BF16)
    bias = jnp.zeros((SGU_CHUNK, W), F32)
    for hh in range(N_GROUPS):
        bias = jnp.where(group == hh, sgub_ref[:, hh:hh + 1], bias)
    window = jnp.zeros((1, W), jnp.int32)
    for gi, kw in enumerate(POOL_WINDOWS):
        window = jnp.where(group == gi, kw, window)
    group_h = _halves(group)

    def project(sb):
        x = x_ref[sb * S:(sb + 1) * S, :]
        h = _rms(x, g_ref[...]).astype(BF16)
        return (x, _dot(h, win_ref[:, 0:3 * W]), _dot(h, win_ref[:, 3 * W:5 * W]),
                _dot(h, win_ref[:, 5 * W:7 * W]), _dot(h, win_ref[:, 7 * W:8 * W]))

    def mix(sb, x, za, zb, zc, d_w):
        r0 = sb * S

        a_b, a_c, a_x = za[:, 0:W], za[:, W:2 * W], za[:, 2 * W:3 * W]
        p_h = _halves(a_c * a_x)
        conv_h = []
        for hf in range(N_HALF):
            lanes = slice(hf * LANES, (hf + 1) * LANES)
            base = SCONV_HALO + r0
            pa_ref[hf, base:base + S, :] = p_h[hf]
            conv = p_h[hf] * sconv_ref[SCONV_K - 1:SCONV_K, lanes]
            for k in range(SCONV_K - 1):
                off = base - (SCONV_K - 1) + k
                conv = conv + pa_ref[hf, off:off + S, :] * sconv_ref[k:k + 1, lanes]
            conv_h.append(conv)
        ycat_ref[r0:r0 + S, 0:W] = (a_b * jnp.concatenate(conv_h, axis=1)).astype(BF16)

        b_u, b_v = zb[:, 0:W], zb[:, W:2 * W]
        vn = _layernorm(b_v, sgug_ref[...])
        for c in range(S // SGU_CHUNK):
            c0 = c * SGU_CHUNK
            vc = vn[c0:c0 + SGU_CHUNK, :]
            stacked = jnp.concatenate(
                [jnp.where(group == hh, vc, 0.0).astype(BF16) for hh in range(N_GROUPS)], axis=0)
            mixed = _dot(w_causal, stacked) + bias
            ycat_ref[r0 + c0:r0 + c0 + SGU_CHUNK, W:2 * W] = (
                b_u[c0:c0 + SGU_CHUNK, :] * mixed).astype(BF16)

        glu_h = _halves(zc[:, 0:W] * jax.nn.sigmoid(zc[:, W:2 * W]))
        for hf in range(N_HALF):
            yc_ref[hf, CCONV_HALO + r0:CCONV_HALO + r0 + S, :] = glu_h[hf]
        for q0 in range(r0, r0 + S, CONV_ROWS):
            acc_h = []
            for hf in range(N_HALF):
                lanes = slice(hf * LANES, (hf + 1) * LANES)
                acc = jnp.zeros((CONV_ROWS, LANES), F32)
                for k in range(CCONV_K):
                    off = CCONV_HALO - (CCONV_K - 1) + k + q0
                    acc = acc + yc_ref[hf, off:off + CONV_ROWS, :] * cconv_ref[k:k + 1, lanes]
                acc_h.append(acc)
            yln = _layernorm(jnp.concatenate(acc_h, axis=1), clng_ref[...], clnb_ref[...])
            ycat_ref[q0:q0 + CONV_ROWS, 2 * W:3 * W] = _silu(yln).astype(BF16)

        dw_h = _halves(d_w)
        wsum_h = []
        for hf in range(N_HALF):
            dw_ref[hf, POOL_HALO + r0:POOL_HALO + r0 + S, :] = dw_h[hf]
            n_lvl = len(POOL_WINDOWS)
            for lvl in range(n_lvl):
                start = 8 * (lvl + 1)
                n = POOL_HALO + S - start
                shift = 2 ** lvl
                if lvl == 0:
                    cur = dw_ref[hf, r0 + start:r0 + start + n, :]
                    back = dw_ref[hf, r0 + start - shift:r0 + start - shift + n, :]
                else:
                    src = lvl_ref.at[sb, hf, (lvl - 1) % 2]
                    cur = src[start:start + n, :]
                    back = src[start - shift:start - shift + n, :]
                nxt = cur + jnp.where(group_h[hf] >= lvl, back, 0.0)
                if lvl == n_lvl - 1:
                    wsum_h.append(nxt)
                else:
                    lvl_ref[sb, hf, lvl % 2, start:start + n, :] = nxt
        wsum = jnp.concatenate(wsum_h, axis=1)
        pos = j * T + r0 + lax.broadcasted_iota(jnp.int32, (S, W), 0)
        count = jnp.minimum(pos + 1, window).astype(F32)
        pooled = (wsum / count - d_w).astype(BF16)
        y_d = _dot(pooled, poolw_ref[...]) * pscale_ref[...]
        ycat_ref[r0:r0 + S, 3 * W:4 * W] = y_d.astype(BF16)

        o_ref[r0:r0 + S, :] = x + _dot(ycat_ref[r0:r0 + S, :], wout_ref[...])

    staged = [project(sb) for sb in range(N_SUB)]
    for sb in range(N_SUB):
        mix(sb, *staged[sb])

    for hf in range(N_HALF):
        pa_ref[hf, 0:SCONV_HALO, :] = pa_ref[hf, T:T + SCONV_HALO, :]
        yc_ref[hf, 0:CCONV_HALO, :] = yc_ref[hf, T:T + CCONV_HALO, :]
        dw_ref[hf, 0:POOL_HALO, :] = dw_ref[hf, T:T + POOL_HALO, :]


def _mixer(x3d, layer, gain, w_in, sconv_w, sgu_g, sgu_w, sgu_b, cconv_w, cln_g, cln_b,
           pool_w, pool_scale, w_out):
    bsz, seq, _ = x3d.shape
    W = MIXER_WIDTH
    T = TOKEN_TILE

    def per_layer(shape):
        return _resident((None,) + shape, lambda b, j: (layer,) + (0,) * len(shape))

    return pl.pallas_call(
        _mixer_kernel,
        out_shape=jax.ShapeDtypeStruct(x3d.shape, F32),
        grid=(bsz, seq // T),
        in_specs=[
            pl.BlockSpec((None, T, D_MODEL), lambda b, j: (b, j, 0)),
            per_layer((1, D_MODEL)),
            per_layer((D_MODEL, 8 * W)),
            per_layer((SCONV_K, W)),
            per_layer((1, W)),
            per_layer((SGU_CHUNK, N_GROUPS * SGU_CHUNK)),
            per_layer((SGU_CHUNK, N_GROUPS)),
            per_layer((CCONV_K, W)),
            per_layer((1, W)),
            per_layer((1, W)),
            per_layer((W, W)),
            per_layer((1, W)),
            per_layer((4 * W, D_MODEL)),
        ],
        out_specs=pl.BlockSpec((None, T, D_MODEL), lambda b, j: (b, j, 0)),
        scratch_shapes=[
            pltpu.VMEM((N_HALF, SCONV_HALO + T, LANES), F32),
            pltpu.VMEM((N_HALF, CCONV_HALO + T, LANES), F32),
            pltpu.VMEM((N_HALF, POOL_HALO + T, LANES), F32),
            pltpu.VMEM((N_SUB, N_HALF, 2, POOL_HALO + SUB_TILE, LANES), F32),
            pltpu.VMEM((T, 4 * W), BF16),
        ],
        compiler_params=pltpu.CompilerParams(
            dimension_semantics=("arbitrary", "arbitrary"), vmem_limit_bytes=VMEM_LIMIT),
        name="mixer",
    )(x3d, gain, w_in, sconv_w, sgu_g, sgu_w, sgu_b, cconv_w, cln_g, cln_b,
      pool_w, pool_scale, w_out)


def _kv_kernel(mem_ref, g_ref, wkv_ref, k_ref, v_ref):
    hm = _rms(mem_ref[...], g_ref[...]).astype(BF16)
    k_ref[...] = _dot(hm, wkv_ref[:, 0:D_MODEL]).astype(BF16)
    v_ref[...] = _dot(hm, wkv_ref[:, D_MODEL:2 * D_MODEL]).astype(BF16)


def _kv(mem, gain, wkv):
    bsz, n_mem, _ = mem.shape
    out = jax.ShapeDtypeStruct((DEPTH, bsz, n_mem, D_MODEL), BF16)
    kv_spec = pl.BlockSpec((None, None, n_mem, D_MODEL), lambda l, b: (l, b, 0, 0))
    return pl.pallas_call(
        _kv_kernel,
        out_shape=(out, out),
        grid=(DEPTH, bsz),
        in_specs=[
            pl.BlockSpec((None, n_mem, D_MODEL), lambda l, b: (b, 0, 0)),
            pl.BlockSpec((None, 1, D_MODEL), lambda l, b: (l, 0, 0)),
            pl.BlockSpec((None, D_MODEL, 2 * D_MODEL), lambda l, b: (l, 0, 0)),
        ],
        out_specs=(kv_spec, kv_spec),
        compiler_params=pltpu.CompilerParams(
            dimension_semantics=("arbitrary", "arbitrary"), vmem_limit_bytes=VMEM_LIMIT),
        name="xattn_kv",
    )(mem, gain, wkv)


def _attn_kernel(x_ref, g_ref, wq_ref, k_ref, v_ref, wo_ref, o_ref, ocat_ref):
    scale = 1.0 / math.sqrt(HEAD_DIM)
    for sb in range(N_SUB):
        rows = slice(sb * SUB_TILE, (sb + 1) * SUB_TILE)
        x = x_ref[rows, :]
        h = _rms(x, g_ref[...]).astype(BF16)
        for hd in range(N_HEADS):
            cols = slice(hd * HEAD_DIM, (hd + 1) * HEAD_DIM)
            q = _dot(h, wq_ref[:, cols]).astype(BF16)
            s = lax.dot_general(q, k_ref[:, cols], (((1,), (1,)), ((), ())),
                                preferred_element_type=F32) * scale
            e = jnp.exp(s - jnp.max(s, axis=-1, keepdims=True))
            p = (e / jnp.sum(e, axis=-1, keepdims=True)).astype(BF16)
            ocat_ref[rows, cols] = _dot(p, v_ref[:, cols]).astype(BF16)
        o_ref[rows, :] = x + _dot(ocat_ref[rows, :], wo_ref[...])


def _attn(x3d, layer, gain, wq, k, v, wo):
    bsz, seq, _ = x3d.shape
    n_mem = k.shape[2]
    T = TOKEN_TILE
    return pl.pallas_call(
        _attn_kernel,
        out_shape=jax.ShapeDtypeStruct(x3d.shape, F32),
        grid=(bsz, seq // T),
        in_specs=[
            pl.BlockSpec((None, T, D_MODEL), lambda b, j: (b, j, 0)),
            _resident((None, 1, D_MODEL), lambda b, j: (layer, 0, 0)),
            _resident((None, D_MODEL, D_MODEL), lambda b, j: (layer, 0, 0)),
            pl.BlockSpec((None, None, n_mem, D_MODEL), lambda b, j: (layer, b, 0, 0)),
            pl.BlockSpec((None, None, n_mem, D_MODEL), lambda b, j: (layer, b, 0, 0)),
            _resident((None, D_MODEL, D_MODEL), lambda b, j: (layer, 0, 0)),
        ],
        out_specs=pl.BlockSpec((None, T, D_MODEL), lambda b, j: (b, j, 0)),
        scratch_shapes=[pltpu.VMEM((T, D_MODEL), BF16)],
        compiler_params=pltpu.CompilerParams(
            dimension_semantics=("arbitrary", "arbitrary"), vmem_limit_bytes=VMEM_LIMIT),
        name="xattn",
    )(x3d, gain, wq, k, v, wo)


def _block_diag(pool_w):
    n_layers, n_g, c, _ = pool_w.shape
    out = jnp.zeros((n_layers, n_g * c, n_g * c), pool_w.dtype)
    for gi in range(n_g):
        out = out.at[:, gi * c:(gi + 1) * c, gi * c:(gi + 1) * c].set(pool_w[:, gi])
    return out


def kernel(x, mem, norm_ffn1, ffn1_w_in, ffn1_w_out, norm_mix, mix_w_in, sconv_w, sgu_norm_g, sgu_w, sgu_b, cconv_w, cconv_ln_g, cconv_ln_b, pool_w, pool_scale, mix_w_out, norm_xattn, norm_mem, xattn_wq, xattn_wkv, xattn_wo, norm_ffn2, ffn2_w_in, ffn2_w_out, norm_final):
    bsz, seq, d = x.shape
    n_layers = norm_ffn1.shape[0]

    def row(a):
        return a.reshape(a.shape[0], 1, a.shape[1])

    ffn1_w_in_b, ffn1_w_out_b = ffn1_w_in.astype(BF16), ffn1_w_out.astype(BF16)
    ffn2_w_in_b, ffn2_w_out_b = ffn2_w_in.astype(BF16), ffn2_w_out.astype(BF16)
    mix_w_in_b, mix_w_out_b = mix_w_in.astype(BF16), mix_w_out.astype(BF16)
    wq_b, wkv_b, wo_b = xattn_wq.astype(BF16), xattn_wkv.astype(BF16), xattn_wo.astype(BF16)
    pool_w_b = _block_diag(pool_w).astype(BF16)
    sgu_w_cat = sgu_w.transpose(0, 2, 1, 3).reshape(n_layers, SGU_CHUNK, N_GROUPS * SGU_CHUNK)
    sgu_b_t = sgu_b.transpose(0, 2, 1)

    k_all, v_all = _kv(mem, row(norm_mem), wkv_b)

    for l in range(n_layers):
        x2d = _ffn(x.reshape(bsz * seq, d), row(norm_ffn1), ffn1_w_in_b, ffn1_w_out_b, l)
        x = x2d.reshape(bsz, seq, d)
        x = _mixer(x, l, row(norm_mix), mix_w_in_b, sconv_w, row(sgu_norm_g), sgu_w_cat,
                   sgu_b_t, cconv_w, row(cconv_ln_g), row(cconv_ln_b), pool_w_b,
                   row(pool_scale), mix_w_out_b)
        x = _attn(x, l, row(norm_xattn), wq_b, k_all, v_all, wo_b)
        last = l == n_layers - 1
        x2d = _ffn(x.reshape(bsz * seq, d), row(norm_ffn2), ffn2_w_in_b, ffn2_w_out_b, l,
                   final_gain=norm_final.reshape(1, d) if last else None)
        x = x2d.reshape(bsz, seq, d)
    return x
```
